```python
import math
import jax, jax.numpy as jnp
from jax import lax
import numpy as np

D_MODEL = 1024
BATCH = 2
SEQ = 8192
DEPTH = 4
DEC_BATCH = 32
DEC_SEQ = 1
PAST_LEN = 8192
PAGE_SIZE = 128

N_HEADS = 8
HEAD_DIM = D_MODEL // N_HEADS
N_IDX_HEADS = 8
IDX_DIM = 64
TOPK_MAX = 256
ROPE_THETA = 10000.0
Q_BLOCK = 128
CHUNK = 128
D_GATE = D_MODEL
N_SG_GROUPS = 8
SG_GROUP_DIM = D_GATE // N_SG_GROUPS
D_FF = 4 * D_MODEL
N_ATTN = (DEPTH + 1) // 2
N_GMLP = DEPTH // 2
EPS = 1e-6
SPLITS = (D_MODEL, 2 * D_MODEL, 3 * D_MODEL,
          3 * D_MODEL + N_IDX_HEADS * IDX_DIM,
          3 * D_MODEL + N_IDX_HEADS * IDX_DIM + IDX_DIM)
ATTN_IN_DIM = 3 * D_MODEL + N_IDX_HEADS * IDX_DIM + IDX_DIM + N_IDX_HEADS

kernel_name = 'dsa_gmlp_hybrid_decode_step'

F32 = jnp.float32


def rms_norm(x, g):
    xf = x.astype(F32)
    y = xf * lax.rsqrt(jnp.mean(xf * xf, axis=-1, keepdims=True) + EPS)
    return (y * g.astype(F32)).astype(x.dtype)


def layer_norm(x, g, b):
    xf = x.astype(F32)
    xc = xf - jnp.mean(xf, axis=-1, keepdims=True)
    var = jnp.mean(xc * xc, axis=-1, keepdims=True)
    return (xc * lax.rsqrt(var + EPS) * g.astype(F32) + b.astype(F32)).astype(x.dtype)


def rope(x, pos):
    half = x.shape[-1] // 2
    inv = ROPE_THETA ** (-jnp.arange(half, dtype=F32) / half)
    ang = pos.astype(F32)[:, None] * inv[None, :]
    cos = jnp.cos(ang)[None, :, None, :]
    sin = jnp.sin(ang)[None, :, None, :]
    xf = x.astype(F32)
    x1, x2 = xf[..., :half], xf[..., half:]
    return jnp.concatenate([x1 * cos - x2 * sin, x2 * cos + x1 * sin], axis=-1).astype(x.dtype)


def dsa_project(h, w_in, pos):
    B, T, _ = h.shape
    z = h @ w_in
    q, k, v, qi, ki, wi = jnp.split(z, SPLITS, axis=-1)
    q = rope(q.reshape(B, T, N_HEADS, HEAD_DIM), pos)
    k = rope(k.reshape(B, T, N_HEADS, HEAD_DIM), pos)
    v = v.reshape(B, T, N_HEADS, HEAD_DIM)
    qi = rope(qi.reshape(B, T, N_IDX_HEADS, IDX_DIM), pos)
    ki = rope(ki[:, :, None, :], pos)[:, :, 0, :]
    wi = wi * ((N_IDX_HEADS * IDX_DIM) ** -0.5)
    return q, k, v, qi, ki, wi


def index_scores(qi, wi, ki):
    s = jnp.einsum('bqhd,bsd->bqhs', qi, ki, preferred_element_type=F32)
    return jnp.einsum('bqhs,bqh->bqs', jax.nn.relu(s), wi.astype(F32))


def sparse_attend(q, k_sel, v_sel, valid):
    logits = jnp.einsum('bqhd,bqkhd->bqhk', q, k_sel, preferred_element_type=F32) * (HEAD_DIM ** -0.5)
    logits = jnp.where(valid[:, :, None, :], logits, -jnp.inf)
    p = jax.nn.softmax(logits, axis=-1).astype(v_sel.dtype)
    return jnp.einsum('bqhk,bqkhd->bqhd', p, v_sel)


def gather_rows(a, idx):
    return jax.vmap(lambda aa, ii: aa[ii])(a, idx)


def dsa_prompt(h, w_in, w_out):
    B, T, _ = h.shape
    pos = jnp.arange(T)
    q, k, v, qi, ki, wi = dsa_project(h, w_in, pos)
    topk = min(TOPK_MAX, T // 4)
    nb = T // Q_BLOCK

    def blocked(a):
        return jnp.swapaxes(a.reshape((B, nb, Q_BLOCK) + a.shape[2:]), 0, 1)

    key_pos = jnp.arange(T)

    def block_fn(args):
        qb, qib, wib, start = args
        qpos = start + jnp.arange(Q_BLOCK)
        causal = key_pos[None, :] <= qpos[:, None]
        score = jnp.where(causal[None], index_scores(qib, wib, ki), -jnp.inf)
        _, idx = lax.top_k(score, topk)
        valid = idx <= qpos[None, :, None]
        return sparse_attend(qb, gather_rows(k, idx), gather_rows(v, idx), valid)

    starts = jnp.arange(nb, dtype=jnp.int32) * Q_BLOCK
    o = lax.map(block_fn, (blocked(q), blocked(qi), blocked(wi), starts))
    o = jnp.swapaxes(o, 0, 1).reshape(B, T, D_MODEL)
    return o @ w_out, k, v, ki


def dsa_sample(h, w_in, w_out, cache_k, cache_v, cache_kidx, layer, page_table):
    DB, DS, _ = h.shape
    n_pages = page_table.shape[1]
    past = n_pages * PAGE_SIZE
    pos = past + jnp.arange(DS)
    q, k, v, qi, ki, wi = dsa_project(h, w_in, pos)
    ki_past = cache_kidx[layer, page_table].reshape(DB, past, IDX_DIM).astype(ki.dtype)
    ki_all = jnp.concatenate([ki_past, ki], axis=1)
    L = past + DS
    topk = min(TOPK_MAX, L // 4)
    causal = jnp.arange(L)[None, :] <= pos[:, None]
    score = jnp.where(causal[None], index_scores(qi, wi, ki_all), -jnp.inf)
    _, idx = lax.top_k(score, topk)
    in_past = (idx < past)[..., None, None]
    page = jnp.minimum(idx // PAGE_SIZE, n_pages - 1)
    phys = jax.vmap(lambda pt, p: pt[p])(page_table, page)
    off = idx % PAGE_SIZE
    new_i = jnp.clip(idx - past, 0, DS - 1)
    k_sel = jnp.where(in_past, cache_k[layer, phys, off].astype(k.dtype), gather_rows(k, new_i))
    v_sel = jnp.where(in_past, cache_v[layer, phys, off].astype(v.dtype), gather_rows(v, new_i))
    valid = idx <= pos[None, :, None]
    o = sparse_attend(q, k_sel, v_sel, valid).reshape(DB, DS, D_MODEL)
    return o @ w_out, k, v, ki


def sgu_project(h, w_in, ln_g, ln_b):
    z = jax.nn.gelu(h @ w_in, approximate=False)
    u, v = jnp.split(z, 2, axis=-1)
    return u, layer_norm(v, ln_g, ln_b)


def causal_spatial(w_s):
    return w_s * jnp.tril(jnp.ones((CHUNK, CHUNK), w_s.dtype))


def gmlp_prompt(h, w_in, ln_g, ln_b, w_s, b_s, w_out):
    B, T, _ = h.shape
    u, v = sgu_project(h, w_in, ln_g, ln_b)
    vc = v.reshape(B, T // CHUNK, CHUNK, N_SG_GROUPS, SG_GROUP_DIM)
    mix = jnp.einsum('gts,bcsgd->bctgd', causal_spatial(w_s), vc) + b_s.T[None, None, :, :, None]
    return (u * mix.reshape(B, T, D_GATE)) @ w_out


def gmlp_sample(h, w_in, ln_g, ln_b, w_s, b_s, w_out):
    DB, DS, _ = h.shape
    u, v = sgu_project(h, w_in, ln_g, ln_b)
    w_m = causal_spatial(w_s)[:, :DS, :DS]
    vg = v.reshape(DB, DS, N_SG_GROUPS, SG_GROUP_DIM)
    mix = jnp.einsum('gts,bsgd->btgd', w_m, vg) + b_s[:, :DS].T[None, :, :, None]
    return (u * mix.reshape(DB, DS, D_GATE)) @ w_out, v


def sq_relu_ffn(h, w1, w2):
    return jnp.square(jax.nn.relu(h @ w1)) @ w2


def setup_inputs(seed: int = 0) -> dict:
    key = jax.random.key(seed)
    ks = jax.random.split(key, 24)
    n_pages = PAST_LEN // PAGE_SIZE
    n_used = DEC_BATCH * n_pages
    n_phys = n_used + n_used // 4
    nrm = jax.random.normal
    page_table = jax.random.permutation(ks[0], n_phys)[:n_used].reshape(DEC_BATCH, n_pages).astype(jnp.int32)
    return {
        'x_prompt': nrm(ks[1], (BATCH, SEQ, D_MODEL), F32),
        'x_sample': nrm(ks[2], (DEC_BATCH, DEC_SEQ, D_MODEL), F32),
        'cache_k': nrm(ks[3], (N_ATTN, n_phys, PAGE_SIZE, N_HEADS, HEAD_DIM), F32),
        'cache_v': nrm(ks[4], (N_ATTN, n_phys, PAGE_SIZE, N_HEADS, HEAD_DIM), F32),
        'cache_kidx': nrm(ks[5], (N_ATTN, n_phys, PAGE_SIZE, IDX_DIM), F32),
        'page_table': page_table,
        'norm_mix_g': 1.0 + 0.02 * nrm(ks[6], (DEPTH, D_MODEL), F32),
        'norm_ffn_g': 1.0 + 0.02 * nrm(ks[7], (DEPTH, D_MODEL), F32),
        'final_norm_g': 1.0 + 0.02 * nrm(ks[8], (D_MODEL,), F32),
        'w_attn_in': nrm(ks[9], (N_ATTN, D_MODEL, ATTN_IN_DIM), F32) * D_MODEL ** -0.5,
        'w_attn_out': nrm(ks[10], (N_ATTN, D_MODEL, D_MODEL), F32) * D_MODEL ** -0.5,
        'w_sg_in': nrm(ks[11], (N_GMLP, D_MODEL, 2 * D_GATE), F32) * D_MODEL ** -0.5,
        'sg_ln_g': 1.0 + 0.02 * nrm(ks[12], (N_GMLP, D_GATE), F32),
        'sg_ln_b': 0.02 * nrm(ks[13], (N_GMLP, D_GATE), F32),
        'w_spatial': nrm(ks[14], (N_GMLP, N_SG_GROUPS, CHUNK, CHUNK), F32) * CHUNK ** -0.5,
        'b_spatial': 1.0 + 0.1 * nrm(ks[15], (N_GMLP, N_SG_GROUPS, CHUNK), F32),
        'w_sg_out': nrm(ks[16], (N_GMLP, D_GATE, D_MODEL), F32) * D_GATE ** -0.5,
        'w_ff1': nrm(ks[17], (DEPTH, D_MODEL, D_FF), F32) * D_MODEL ** -0.5,
        'w_ff2': nrm(ks[18], (DEPTH, D_FF, D_MODEL), F32) * D_FF ** -0.5,
    }


def reference(x_prompt, x_sample, cache_k, cache_v, cache_kidx, page_table,
              norm_mix_g, norm_ffn_g, final_norm_g, w_attn_in, w_attn_out,
              w_sg_in, sg_ln_g, sg_ln_b, w_spatial, b_spatial, w_sg_out, w_ff1, w_ff2):
    xp, xs = x_prompt, x_sample
    kp, vp, kip, ksm, vsm, kism, svs = [], [], [], [], [], [], []
    for i in range(DEPTH):
        j = i // 2
        hp = rms_norm(xp, norm_mix_g[i])
        hs = rms_norm(xs, norm_mix_g[i])
        if i % 2 == 0:
            yp, k_p, v_p, ki_p = dsa_prompt(hp, w_attn_in[j], w_attn_out[j])
            ys, k_s, v_s, ki_s = dsa_sample(hs, w_attn_in[j], w_attn_out[j],
                                            cache_k, cache_v, cache_kidx, j, page_table)
            kp.append(k_p); vp.append(v_p); kip.append(ki_p)
            ksm.append(k_s); vsm.append(v_s); kism.append(ki_s)
        else:
            yp = gmlp_prompt(hp, w_sg_in[j], sg_ln_g[j], sg_ln_b[j], w_spatial[j], b_spatial[j], w_sg_out[j])
            ys, sv = gmlp_sample(hs, w_sg_in[j], sg_ln_g[j], sg_ln_b[j], w_spatial[j], b_spatial[j], w_sg_out[j])
            svs.append(sv)
        xp = xp + yp
        xs = xs + ys
        xp = xp + sq_relu_ffn(rms_norm(xp, norm_ffn_g[i]), w_ff1[i], w_ff2[i])
        xs = xs + sq_relu_ffn(rms_norm(xs, norm_ffn_g[i]), w_ff1[i], w_ff2[i])
    y_prompt = rms_norm(xp, final_norm_g)
    y_sample = rms_norm(xs, final_norm_g)
    return (y_prompt, y_sample, jnp.stack(kp), jnp.stack(vp), jnp.stack(kip),
            jnp.stack(ksm), jnp.stack(vsm), jnp.stack(kism), jnp.stack(svs))
```

```python
import functools
import math

import jax
import jax.numpy as jnp
import numpy as np
from jax import lax
from jax.experimental import pallas as pl
from jax.experimental.pallas import tpu as pltpu

F32 = jnp.float32
BF16 = jnp.bfloat16
I32 = jnp.int32

N_HEADS = 8
HEAD_DIM = 128
N_IDX_HEADS = 8
IDX_DIM = 64
TOPK_MAX = 256
ROPE_THETA = 10000.0
PAGE_SIZE = 128
CHUNK = 128
N_SG_GROUPS = 8
EPS = 1e-6

LANES = 128
INT_MIN = -(2 ** 31)
NEG_BIG = -1e30
VMEM_LIMIT = 48 * 1024 * 1024

_NT = (((1,), (1,)), ((), ()))


def _dot(a, b):
    return jnp.dot(a, b, preferred_element_type=F32)


def _dot_nt(a, b):
    return lax.dot_general(a, b, _NT, preferred_element_type=F32)


def _rms(x, g):
    y = x * lax.rsqrt(jnp.mean(x * x, axis=-1, keepdims=True) + EPS)
    return y * g


def _cparams(sem):
    return pltpu.CompilerParams(dimension_semantics=sem, vmem_limit_bytes=VMEM_LIMIT)


def _rope_tables(pos):
    pos = pos.astype(F32)[:, None]
    lane = jnp.arange(LANES)

    half = HEAD_DIM // 2
    inv = ROPE_THETA ** (-jnp.arange(half, dtype=F32) / half)
    ang = pos * inv[None, :]
    cos_h = jnp.concatenate([jnp.cos(ang), jnp.cos(ang)], axis=1)
    sin_h = jnp.concatenate([-jnp.sin(ang), jnp.sin(ang)], axis=1)

    half_i = IDX_DIM // 2
    inv_i = ROPE_THETA ** (-jnp.arange(half_i, dtype=F32) / half_i)
    ang_i = pos * inv_i[None, :]
    cos_i = jnp.tile(jnp.cos(ang_i), (1, LANES // half_i))
    sin_i = jnp.tile(jnp.sin(ang_i), (1, LANES // half_i))
    first = ((lane % IDX_DIM) < half_i)[None, :]
    sin_a = jnp.where(first, -sin_i, 0.0)
    sin_b = jnp.where(first, 0.0, sin_i)

    is_k = (lane < IDX_DIM)[None, :]
    is_w = ((lane >= IDX_DIM) & (lane < IDX_DIM + N_IDX_HEADS))[None, :]
    w_scale = float((N_IDX_HEADS * IDX_DIM) ** -0.5)
    cos_k = jnp.where(is_k, cos_i, jnp.where(is_w, w_scale, 0.0))
    sin_ka = jnp.where(is_k, sin_a, 0.0)
    sin_kb = jnp.where(is_k, sin_b, 0.0)
    return jnp.stack([cos_h, sin_h, cos_i, sin_a, sin_b, cos_k, sin_ka, sin_kb]).astype(F32)


def _proj_rope_body(x_ref, g_ref, wqkv_ref, wqi_ref, wkw_ref, tab_ref,
                    q_ref, k_ref, v_ref, kb_ref, vb_ref, qi_ref, kw_ref, kwb_ref):
    d_model = x_ref.shape[1]
    hn = _rms(x_ref[...], g_ref[...]).astype(BF16)
    cos_h, sin_h = tab_ref[0], tab_ref[1]
    for h in range(N_HEADS):
        sl = slice(h * HEAD_DIM, (h + 1) * HEAD_DIM)
        zq = _dot(hn, wqkv_ref[:, sl])
        q_ref[:, sl] = (zq * cos_h + pltpu.roll(zq, HEAD_DIM // 2, 1) * sin_h).astype(BF16)
        zk = _dot(hn, wqkv_ref[:, d_model + h * HEAD_DIM:d_model + (h + 1) * HEAD_DIM])
        kr = zk * cos_h + pltpu.roll(zk, HEAD_DIM // 2, 1) * sin_h
        k_ref[:, sl] = kr
        kb_ref[:, sl] = kr.astype(BF16)
        zv = _dot(hn, wqkv_ref[:, 2 * d_model + h * HEAD_DIM:2 * d_model + (h + 1) * HEAD_DIM])
        v_ref[:, sl] = zv
        vb_ref[:, sl] = zv.astype(BF16)

    def rope_idx(z, c, sa, sb):
        return (z * c + pltpu.roll(z, LANES - IDX_DIM // 2, 1) * sa
                + pltpu.roll(z, IDX_DIM // 2, 1) * sb)

    cos_i, sin_a, sin_b = tab_ref[2], tab_ref[3], tab_ref[4]
    for j in range(N_IDX_HEADS * IDX_DIM // LANES):
        sl = slice(j * LANES, (j + 1) * LANES)
        qi_ref[:, sl] = rope_idx(_dot(hn, wqi_ref[:, sl]), cos_i, sin_a, sin_b).astype(BF16)
    kw = rope_idx(_dot(hn, wkw_ref[...]), tab_ref[5], tab_ref[6], tab_ref[7])
    kw_ref[...] = kw
    kwb_ref[...] = kw.astype(BF16)


def _proj_rope(x2d, g, wqkv, wqi, wkw, tab, tm):
    n, d = x2d.shape
    nt = tab.shape[1] // tm
    row = lambda i: (i, 0)
    const = lambda i: (0, 0)
    n_qi = N_IDX_HEADS * IDX_DIM
    out_shape = (
        jax.ShapeDtypeStruct((n, d), BF16),
        jax.ShapeDtypeStruct((n, d), F32),
        jax.ShapeDtypeStruct((n, d), F32),
        jax.ShapeDtypeStruct((n, d), BF16),
        jax.ShapeDtypeStruct((n, d), BF16),
        jax.ShapeDtypeStruct((n, n_qi), BF16),
        jax.ShapeDtypeStruct((n, LANES), F32),
        jax.ShapeDtypeStruct((n, LANES), BF16),
    )
    return pl.pallas_call(
        _proj_rope_body,
        grid=(n // tm,),
        in_specs=[
            pl.BlockSpec((tm, d), row),
            pl.BlockSpec((1, d), const),
            pl.BlockSpec(wqkv.shape, const),
            pl.BlockSpec(wqi.shape, const),
            pl.BlockSpec(wkw.shape, const),
            pl.BlockSpec((8, tm, LANES), lambda i: (0, i % nt, 0)),
        ],
        out_specs=(
            pl.BlockSpec((tm, d), row), pl.BlockSpec((tm, d), row), pl.BlockSpec((tm, d), row),
            pl.BlockSpec((tm, d), row), pl.BlockSpec((tm, d), row),
            pl.BlockSpec((tm, n_qi), row), pl.BlockSpec((tm, LANES), row),
            pl.BlockSpec((tm, LANES), row),
        ),
        out_shape=out_shape,
        compiler_params=_cparams(("arbitrary",)),
        name="dsa_proj_rope",
    )(x2d, g.reshape(1, d), wqkv, wqi, wkw, tab)


def _post_ffn_body(x_ref, o_ref, wo_ref, g_ref, w1_ref, w2_ref, out_ref, x1_scr, hn_scr, acc_scr):
    f = pl.program_id(1)

    @pl.when(f == 0)
    def _():
        x1 = x_ref[...] + _dot(o_ref[...], wo_ref[...])
        x1_scr[...] = x1
        hn_scr[...] = _rms(x1, g_ref[...]).astype(BF16)
        acc_scr[...] = jnp.zeros_like(acc_scr)

    h1 = jnp.square(jnp.maximum(_dot(hn_scr[...], w1_ref[...]), 0.0)).astype(BF16)
    acc_scr[...] += _dot(h1, w2_ref[...])

    @pl.when(f == pl.num_programs(1) - 1)
    def _():
        out_ref[...] = x1_scr[...] + acc_scr[...]


def _post_ffn(x2d, o2d, wo, g, w1, w2, tm, tf):
    n, d = x2d.shape
    dff = w1.shape[1]
    row = lambda i, f: (i, 0)
    return pl.pallas_call(
        _post_ffn_body,
        grid=(n // tm, dff // tf),
        in_specs=[
            pl.BlockSpec((tm, d), row),
            pl.BlockSpec((tm, d), row),
            pl.BlockSpec(wo.shape, lambda i, f: (0, 0)),
            pl.BlockSpec((1, d), lambda i, f: (0, 0)),
            pl.BlockSpec((d, tf), lambda i, f: (0, f)),
            pl.BlockSpec((tf, d), lambda i, f: (f, 0)),
        ],
        out_specs=pl.BlockSpec((tm, d), row),
        out_shape=jax.ShapeDtypeStruct((n, d), F32),
        scratch_shapes=[pltpu.VMEM((tm, d), F32), pltpu.VMEM((tm, d), BF16),
                        pltpu.VMEM((tm, d), F32)],
        compiler_params=_cparams(("arbitrary", "arbitrary")),
        name="outproj_ffn",
    )(x2d, o2d, wo, g.reshape(1, d), w1, w2)


def _sgu_core(x_ref, g_ref, win_ref, lng_ref, lnb_ref):
    d_gate = lng_ref.shape[1]
    hn = _rms(x_ref[...], g_ref[...]).astype(BF16)
    z = _dot(hn, win_ref[...])
    z = 0.5 * z * (1.0 + lax.erf(z * np.float32(math.sqrt(0.5))))
    u, v = z[:, :d_gate], z[:, d_gate:]
    vc = v - jnp.mean(v, axis=-1, keepdims=True)
    var = jnp.mean(vc * vc, axis=-1, keepdims=True)
    vn = vc * lax.rsqrt(var + EPS) * lng_ref[...] + lnb_ref[...]
    return u, vn


def _sgu_prompt_body(x_ref, g_ref, win_ref, lng_ref, lnb_ref, ws_ref, bs_ref, o_ref):
    u, vn = _sgu_core(x_ref, g_ref, win_ref, lng_ref, lnb_ref)
    vb = vn.astype(BF16)
    gd = vn.shape[1] // N_SG_GROUPS
    r = lax.broadcasted_iota(I32, (CHUNK, CHUNK), 0)
    c = lax.broadcasted_iota(I32, (CHUNK, CHUNK), 1)
    tril = c <= r
    for g in range(N_SG_GROUPS):
        wm = jnp.where(tril, ws_ref[g], 0.0).astype(BF16)
        gs = slice(g * gd, (g + 1) * gd)
        for ch in range(x_ref.shape[0] // CHUNK):
            rs = slice(ch * CHUNK, (ch + 1) * CHUNK)
            mix = _dot(wm, vb[rs, gs]) + bs_ref[:, gs]
            o_ref[rs, gs] = (u[rs, gs] * mix).astype(BF16)


def _sgu_prompt(x2d, g, win, lng, lnb, ws, bs_full, tm):
    n, d = x2d.shape
    dg = lng.shape[0]
    row = lambda i: (i, 0)
    const = lambda i: (0, 0)
    return pl.pallas_call(
        _sgu_prompt_body,
        grid=(n // tm,),
        in_specs=[
            pl.BlockSpec((tm, d), row),
            pl.BlockSpec((1, d), const),
            pl.BlockSpec(win.shape, const),
            pl.BlockSpec((1, dg), const),
            pl.BlockSpec((1, dg), const),
            pl.BlockSpec(ws.shape, lambda i: (0, 0, 0)),
            pl.BlockSpec(bs_full.shape, const),
        ],
        out_specs=pl.BlockSpec((tm, dg), row),
        out_shape=jax.ShapeDtypeStruct((n, dg), BF16),
        compiler_params=_cparams(("arbitrary",)),
        name="sgu_prompt",
    )(x2d, g.reshape(1, d), win, lng.reshape(1, dg), lnb.reshape(1, dg), ws, bs_full)


def _sgu_sample_body(x_ref, g_ref, win_ref, lng_ref, lnb_ref, w0_ref, b0_ref, o_ref, vn_ref):
    u, vn = _sgu_core(x_ref, g_ref, win_ref, lng_ref, lnb_ref)
    vn_ref[...] = vn
    o_ref[...] = (u * (w0_ref[...] * vn + b0_ref[...])).astype(BF16)


def _sgu_sample(x2d, g, win, lng, lnb, w0_row, b0_row):
    n, d = x2d.shape
    dg = lng.shape[0]
    full = lambda a: pl.BlockSpec(a.shape, lambda i: (0,) * a.ndim)
    args = (x2d, g.reshape(1, d), win, lng.reshape(1, dg), lnb.reshape(1, dg),
            w0_row.reshape(1, dg), b0_row.reshape(1, dg))
    return pl.pallas_call(
        _sgu_sample_body,
        grid=(1,),
        in_specs=[full(a) for a in args],
        out_specs=(pl.BlockSpec((n, dg), lambda i: (0, 0)), pl.BlockSpec((n, dg), lambda i: (0, 0))),
        out_shape=(jax.ShapeDtypeStruct((n, dg), BF16), jax.ShapeDtypeStruct((n, dg), F32)),
        compiler_params=_cparams(("arbitrary",)),
        name="sgu_sample",
    )(*args)


def _norm_body(x_ref, g_ref, o_ref):
    o_ref[...] = _rms(x_ref[...], g_ref[...])


def _final_norm(x2d, g, tm):
    n, d = x2d.shape
    return pl.pallas_call(
        _norm_body,
        grid=(n // tm,),
        in_specs=[pl.BlockSpec((tm, d), lambda i: (i, 0)), pl.BlockSpec((1, d), lambda i: (0, 0))],
        out_specs=pl.BlockSpec((tm, d), lambda i: (i, 0)),
        out_shape=jax.ShapeDtypeStruct((n, d), F32),
        compiler_params=_cparams(("arbitrary",)),
        name="final_norm",
    )(x2d, g.reshape(1, d))


def _score_keys(score):
    bits = lax.bitcast_convert_type(score, I32)
    return jnp.where(bits >= 0, bits, INT_MIN - bits)


def _bit(n):
    return jnp.left_shift(jnp.int32(1), n)


def _dsa_prompt_body(qb_tab, kb_tab, q_ref, qi_ref, kw_ref, kiall_ref, k_ref, v_ref, o_ref,
                     keys_scr, m_scr, l_scr, acc_scr, *, tq, tk, topk, seq):
    p = pl.program_id(1)
    qb = qb_tab[p]
    kb = kb_tab[p]
    n_chunks = (qb * tq + tq + tk - 1) // tk
    idx_bits = max(1, (seq - 1).bit_length())

    @pl.when(kb == 0)
    def _select():
        wi = kw_ref[:, IDX_DIM:IDX_DIM + N_IDX_HEADS]
        qpos = qb * tq + lax.broadcasted_iota(I32, (tq, tk), 0)
        lane = lax.broadcasted_iota(I32, (tq, tk), 1)

        def score_chunk(c, carry):
            s0 = pl.multiple_of(c * tk, tk)
            kic = kiall_ref[pl.ds(s0, tk), :][:, :IDX_DIM]
            acc = jnp.zeros((tq, tk), F32)
            for h in range(N_IDX_HEADS):
                s = _dot_nt(qi_ref[:, h * IDX_DIM:(h + 1) * IDX_DIM], kic)
                acc = acc + jnp.maximum(s, 0.0) * wi[:, h:h + 1]
            keys_scr[:, pl.ds(s0, tk)] = jnp.where(lane + s0 <= qpos, _score_keys(acc), INT_MIN)
            return carry

        lax.fori_loop(0, n_chunks, score_chunk, 0)

        def count(pred):
            def body(c, cnt):
                s0 = pl.multiple_of(c * tk, tk)
                hit = pred(keys_scr[:, pl.ds(s0, tk)], s0).astype(I32)
                for j in range(tk // LANES):
                    cnt = cnt + hit[:, j * LANES:(j + 1) * LANES]
                return cnt

            cnt = lax.fori_loop(0, n_chunks, body, jnp.zeros((tq, LANES), I32))
            return jnp.sum(cnt, axis=1, keepdims=True)

        c0 = count(lambda blk, s0: blk >= 0)
        lo = jnp.where(c0 >= topk, 0, INT_MIN).astype(I32)

        def bit_step(i, lo):
            cand = lo + _bit(30 - i)
            cnt = count(lambda blk, s0: blk >= cand)
            return jnp.where(cnt >= topk, cand, lo)

        lo = lax.fori_loop(0, 31, bit_step, lo)

        cnt_gt = count(lambda blk, s0: blk > lo)
        cnt_ge = count(lambda blk, s0: blk >= lo)
        need = topk - cnt_gt
        split = jnp.where(cnt_ge > topk, jnp.where(lo > INT_MIN, 1, 0), 0)

        def tie_search():
            def step(i, pfx):
                cand = pfx + _bit(idx_bits - 1 - i)
                cnt = count(lambda blk, s0: jnp.where(blk == lo, lane + s0, seq) < cand)
                return jnp.where(cnt < need, cand, pfx)

            return lax.fori_loop(0, idx_bits, step, jnp.zeros((tq, 1), I32))

        cut = lax.cond(jnp.max(split) > 0, tie_search, lambda: jnp.full((tq, 1), seq, I32))
        cut = jnp.where(split > 0, cut, seq)

        def bias_chunk(c, carry):
            s0 = pl.multiple_of(c * tk, tk)
            blk = keys_scr[:, pl.ds(s0, tk)]
            tie = jnp.where(lane + s0 <= cut, 0.0, NEG_BIG)
            bias = jnp.where(blk > lo, 0.0, jnp.where(blk == lo, tie, NEG_BIG))
            bias = jnp.where(blk == INT_MIN, NEG_BIG, bias).astype(F32)
            keys_scr[:, pl.ds(s0, tk)] = lax.bitcast_convert_type(bias, I32)
            return carry

        lax.fori_loop(0, n_chunks, bias_chunk, 0)
        m_scr[...] = jnp.full(m_scr.shape, NEG_BIG, F32)
        l_scr[...] = jnp.zeros(l_scr.shape, F32)
        acc_scr[...] = jnp.zeros(acc_scr.shape, F32)

    bias = lax.bitcast_convert_type(keys_scr[:, pl.ds(pl.multiple_of(kb * tk, tk), tk)], F32)
    scale = np.float32(HEAD_DIM ** -0.5)
    for h in range(N_HEADS):
        hs = slice(h * HEAD_DIM, (h + 1) * HEAD_DIM)
        s = _dot_nt(q_ref[:, hs], k_ref[:, hs]) * scale + bias
        m_old = m_scr[h]
        m_new = jnp.maximum(m_old, jnp.max(s, axis=1, keepdims=True))
        alpha = jnp.exp(m_old - m_new)
        pexp = jnp.exp(s - jnp.concatenate([m_new] * (tk // LANES), axis=1))
        l_scr[h] = alpha * l_scr[h] + jnp.sum(pexp, axis=1, keepdims=True)
        acc_scr[:, hs] = alpha * acc_scr[:, hs] + _dot(pexp.astype(BF16), v_ref[:, hs])
        m_scr[h] = m_new

    @pl.when(kb == n_chunks - 1)
    def _finish():
        for h in range(N_HEADS):
            hs = slice(h * HEAD_DIM, (h + 1) * HEAD_DIM)
            o_ref[:, hs] = (acc_scr[:, hs] / l_scr[h]).astype(BF16)


def _dsa_prompt_attn(q, qi, kw, kwb, kb16, vb16, batch, seq, tq, tk, topk):
    d = q.shape[1]
    nqb, nkb = seq // tq, seq // tk
    pairs = [(a, b) for a in range(nqb) for b in range(((a + 1) * tq - 1) // tk + 1)]
    qb_tab = jnp.asarray([a for a, _ in pairs], I32)
    kb_tab = jnp.asarray([b for _, b in pairs], I32)
    qrow = lambda b, p, qt, kt: (b * nqb + qt[p], 0)
    krow = lambda b, p, qt, kt: (b * nkb + kt[p], 0)
    grid_spec = pltpu.PrefetchScalarGridSpec(
        num_scalar_prefetch=2,
        grid=(batch, len(pairs)),
        in_specs=[
            pl.BlockSpec((tq, d), qrow),
            pl.BlockSpec((tq, qi.shape[1]), qrow),
            pl.BlockSpec((tq, LANES), qrow),
            pl.BlockSpec((seq, LANES), lambda b, p, qt, kt: (b, 0)),
            pl.BlockSpec((tk, d), krow),
            pl.BlockSpec((tk, d), krow),
        ],
        out_specs=pl.BlockSpec((tq, d), qrow),
        scratch_shapes=[
            pltpu.VMEM((tq, seq), I32),
            pltpu.VMEM((N_HEADS, tq, LANES), F32),
            pltpu.VMEM((N_HEADS, tq, LANES), F32),
            pltpu.VMEM((tq, d), F32),
        ],
    )
    return pl.pallas_call(
        functools.partial(_dsa_prompt_body, tq=tq, tk=tk, topk=topk, seq=seq),
        grid_spec=grid_spec,
        out_shape=jax.ShapeDtypeStruct((batch * seq, d), BF16),
        compiler_params=_cparams(("arbitrary", "arbitrary")),
        name="dsa_prompt_attn",
    )(qb_tab, kb_tab, q, qi, kw, kwb, kb16, vb16)


def _sample_scores_body(pt_ref, qi_ref, wi_ref, *rest, n_pg):
    page_refs, out_ref = rest[:n_pg], rest[n_pg]
    qi = qi_ref[...]
    wi = wi_ref[...]
    for j in range(n_pg):
        s = _dot_nt(qi, page_refs[j][...].astype(BF16))
        out_ref[j:j + 1, :] = jnp.sum(jnp.maximum(s, 0.0) * wi, axis=0, keepdims=True)


def _sample_scores(page_table, qi3, wi3, cache_kidx, layer, n_pg):
    db, n_pages = page_table.shape

    def page_spec(j):
        return pl.BlockSpec((None, None, PAGE_SIZE, IDX_DIM),
                            lambda b, g, pt: (layer, pt[b, g * n_pg + j], 0, 0))

    grid_spec = pltpu.PrefetchScalarGridSpec(
        num_scalar_prefetch=1,
        grid=(db, n_pages // n_pg),
        in_specs=[
            pl.BlockSpec((None, N_IDX_HEADS, IDX_DIM), lambda b, g, pt: (b, 0, 0)),
            pl.BlockSpec((None, N_IDX_HEADS, 1), lambda b, g, pt: (b, 0, 0)),
        ] + [page_spec(j) for j in range(n_pg)],
        out_specs=pl.BlockSpec((None, n_pg, PAGE_SIZE), lambda b, g, pt: (b, g, 0)),
    )
    return pl.pallas_call(
        functools.partial(_sample_scores_body, n_pg=n_pg),
        grid_spec=grid_spec,
        out_shape=jax.ShapeDtypeStruct((db, n_pages, PAGE_SIZE), F32),
        compiler_params=_cparams(("arbitrary", "arbitrary")),
        name="sample_scores",
    )(page_table, qi3, wi3, *([cache_kidx] * n_pg))


def _sample_select_body(sc_ref, qi_ref, kit_ref, kw_ref, idx_ref, keys_scr, c_scr, *, topk):
    rows, width = sc_ref.shape
    idx_bits = max(1, (width - 1).bit_length())

    prod = qi_ref[...].astype(F32) * kit_ref[...].astype(F32)
    head_of = lax.broadcasted_iota(I32, prod.shape, 1) // IDX_DIM
    s_new = jnp.zeros((rows, 1), F32)
    for h in range(N_IDX_HEADS):
        qk = jnp.sum(jnp.where(head_of == h, prod, 0.0), axis=1, keepdims=True)
        s_new = s_new + jnp.maximum(qk, 0.0) * kw_ref[:, IDX_DIM + h:IDX_DIM + h + 1]
    key_new = _score_keys(s_new)
    keys_scr[...] = _score_keys(sc_ref[...])

    def count(past_hit, new_hit):
        return jnp.sum(past_hit.astype(I32), axis=1, keepdims=True) + new_hit.astype(I32)

    c0 = count(keys_scr[...] >= 0, key_new >= 0)
    lo = jnp.where(c0 >= topk, 0, INT_MIN).astype(I32)

    def bit_step(i, lo):
        cand = lo + _bit(30 - i)
        cnt = count(keys_scr[...] >= cand, key_new >= cand)
        return jnp.where(cnt >= topk, cand, lo)

    lo = lax.fori_loop(0, 31, bit_step, lo)
    need = topk - count(keys_scr[...] > lo, key_new > lo)
    idx = lax.broadcasted_iota(I32, (rows, width), 1)

    def tie_step(i, pfx):
        cand = pfx + _bit(idx_bits - 1 - i)
        hit = jnp.where(keys_scr[...] == lo, idx, width) < cand
        cnt = jnp.sum(hit.astype(I32), axis=1, keepdims=True)
        return jnp.where(cnt < need, cand, pfx)

    cut = lax.fori_loop(0, idx_bits, tie_step, jnp.zeros((rows, 1), I32))
    keys = keys_scr[...]
    tie = jnp.where(idx <= cut, 1, 0)
    keys_scr[...] = jnp.where(keys > lo, 1, jnp.where(keys == lo, tie, 0)).astype(I32)

    r_i = lax.broadcasted_iota(I32, (LANES, LANES), 0)
    c_i = lax.broadcasted_iota(I32, (LANES, LANES), 1)
    upper = (r_i <= c_i).astype(BF16)

    ones = jnp.ones((LANES, LANES), BF16)

    def csum(j, off):
        sl = pl.ds(pl.multiple_of(j * LANES, LANES), LANES)
        mb = keys_scr[:, sl].astype(BF16)
        c_scr[:, sl] = _dot(mb, upper) + off
        return off + _dot(mb, ones)

    lax.fori_loop(0, width // LANES, csum, jnp.zeros((rows, LANES), F32))

    r_col = lax.broadcasted_iota(I32, (topk, LANES), 0).astype(F32)
    lane_b = lax.broadcasted_iota(I32, (topk, LANES), 1)

    sub = 8

    def per_group(gi, out):
        g0 = pl.multiple_of(gi * sub, sub)
        for i in range(sub):
            def blk(j, acc, i=i):
                c8 = c_scr[pl.ds(g0, sub), pl.ds(pl.multiple_of(j * LANES, LANES), LANES)]
                return acc + (c8[i:i + 1, :] <= r_col).astype(F32)

            acc = lax.fori_loop(0, width // LANES, blk, jnp.zeros((topk, LANES), F32))
            col = jnp.sum(acc, axis=1, keepdims=True).astype(I32)
            out = jnp.where(lane_b == g0 + i, col, out)
        return out

    idx_ref[...] = lax.fori_loop(0, rows // sub, per_group, jnp.zeros((topk, LANES), I32))


def _sample_select(scores, qi, ki_tiled, kw, topk):
    rows, width = scores.shape
    args = (scores, qi, ki_tiled, kw)
    return pl.pallas_call(
        functools.partial(_sample_select_body, topk=topk),
        grid=(1,),
        in_specs=[pl.BlockSpec(a.shape, lambda i: (0, 0)) for a in args],
        out_specs=pl.BlockSpec((topk, LANES), lambda i: (0, 0)),
        out_shape=jax.ShapeDtypeStruct((topk, LANES), I32),
        scratch_shapes=[pltpu.VMEM((rows, width), I32), pltpu.VMEM((rows, width), F32)],
        compiler_params=_cparams(("arbitrary",)),
        name="sample_select",
    )(*args)


def _gather_rows_body(pt_ref, idx_ref, ck_ref, cv_ref, kn_ref, vn_ref, ko_ref, vo_ref, sem,
                      *, layer, past, topk):
    b = pl.program_id(0)

    def copies(r):
        i = idx_ref[b, r]
        is_past = i < past
        ic = jnp.minimum(i, past - 1)
        phys = pt_ref[b, ic // PAGE_SIZE]
        off = ic % PAGE_SIZE
        old = (pltpu.make_async_copy(ck_ref.at[layer, phys, off], ko_ref.at[0, r], sem.at[0]),
               pltpu.make_async_copy(cv_ref.at[layer, phys, off], vo_ref.at[0, r], sem.at[1]))
        new = (pltpu.make_async_copy(kn_ref.at[b], ko_ref.at[0, r], sem.at[0]),
               pltpu.make_async_copy(vn_ref.at[b], vo_ref.at[0, r], sem.at[1]))
        return is_past, old, new

    def start(r, carry):
        is_past, old, new = copies(r)

        @pl.when(is_past)
        def _():
            old[0].start()
            old[1].start()

        @pl.when(jnp.logical_not(is_past))
        def _():
            new[0].start()
            new[1].start()

        return carry

    def wait(r, carry):
        is_past, old, new = copies(r)

        @pl.when(is_past)
        def _():
            old[0].wait()
            old[1].wait()

        @pl.when(jnp.logical_not(is_past))
        def _():
            new[0].wait()
            new[1].wait()

        return carry

    lax.fori_loop(0, topk, start, 0)
    lax.fori_loop(0, topk, wait, 0)


def _gather_rows(page_table, idx, cache_k, cache_v, k_new, v_new, layer, topk):
    db, n_pages = page_table.shape
    row_shape = cache_k.shape[3:]
    any_spec = pl.BlockSpec(memory_space=pl.ANY)
    out_spec = pl.BlockSpec((1, topk) + row_shape, lambda b, pt, ix: (b, 0, 0, 0))
    grid_spec = pltpu.PrefetchScalarGridSpec(
        num_scalar_prefetch=2,
        grid=(db,),
        in_specs=[any_spec, any_spec, any_spec, any_spec],
        out_specs=(out_spec, out_spec),
        scratch_shapes=[pltpu.SemaphoreType.DMA((2,))],
    )
    out = jax.ShapeDtypeStruct((db, topk) + row_shape, cache_k.dtype)
    return pl.pallas_call(
        functools.partial(_gather_rows_body, layer=layer, past=n_pages * PAGE_SIZE, topk=topk),
        grid_spec=grid_spec,
        out_shape=(out, out),
        compiler_params=_cparams(("arbitrary",)),
        name="sample_gather",
    )(page_table, idx, cache_k, cache_v, k_new, v_new)


def _sample_attend_body(q_ref, k_ref, v_ref, o_ref):
    q = q_ref[...].astype(F32)
    s = jnp.sum(k_ref[...] * q[None], axis=2, keepdims=True) * np.float32(HEAD_DIM ** -0.5)
    pexp = jnp.exp(s - jnp.max(s, axis=0, keepdims=True))
    prob = pexp / jnp.sum(pexp, axis=0, keepdims=True)
    o_ref[...] = jnp.sum(prob * v_ref[...], axis=0).astype(BF16)


def _sample_attend(q3, ksel, vsel):
    db, topk = ksel.shape[:2]
    row_shape = ksel.shape[2:]
    vec = pl.BlockSpec((None,) + row_shape, lambda b: (b, 0, 0))
    sel = pl.BlockSpec((None, topk) + row_shape, lambda b: (b, 0, 0, 0))
    return pl.pallas_call(
        _sample_attend_body,
        grid=(db,),
        in_specs=[vec, sel, sel],
        out_specs=vec,
        out_shape=jax.ShapeDtypeStruct((db,) + row_shape, BF16),
        compiler_params=_cparams(("arbitrary",)),
        name="sample_attend",
    )(q3, ksel, vsel)


def kernel(x_prompt, x_sample, cache_k, cache_v, cache_kidx, page_table, norm_mix_g, norm_ffn_g,
           final_norm_g, w_attn_in, w_attn_out, w_sg_in, sg_ln_g, sg_ln_b, w_spatial, b_spatial,
           w_sg_out, w_ff1, w_ff2):
    batch, seq, d = x_prompt.shape
    db, ds, _ = x_sample.shape
    assert ds == 1 and d == N_HEADS * HEAD_DIM
    depth = norm_mix_g.shape[0]
    n_pages = page_table.shape[1]
    past = n_pages * PAGE_SIZE
    n_qi = N_IDX_HEADS * IDX_DIM
    topk_p = min(TOPK_MAX, seq // 4)
    topk_s = min(TOPK_MAX, (past + ds) // 4)

    xp = x_prompt.reshape(batch * seq, d)
    xs = x_sample.reshape(db, d)
    tab_p = _rope_tables(jnp.arange(seq))
    tab_s = _rope_tables(jnp.full((db,), past))

    tm_proj, tm_ffn, tf_ffn, tm_sgu, tq, tk = 256, 512, 1024, 256, 256, 512
    kp, vp, kip, ksm, vsm, kism, svs = [], [], [], [], [], [], []
    for i in range(depth):
        j = i // 2
        w1 = w_ff1[i].astype(BF16)
        w2 = w_ff2[i].astype(BF16)
        if i % 2 == 0:
            w_in = w_attn_in[j]
            wqkv = w_in[:, :3 * d].astype(BF16)
            wqi = w_in[:, 3 * d:3 * d + n_qi].astype(BF16)
            wkw = w_in[:, 3 * d + n_qi:]
            wkw = jnp.pad(wkw, ((0, 0), (0, LANES - wkw.shape[1]))).astype(BF16)
            wo = w_attn_out[j].astype(BF16)

            _q, _k, _v, _kb, _vb, _qi, _kw, _kwb = _proj_rope(
                xp, norm_mix_g[i], wqkv, wqi, wkw, tab_p, tm_proj)
            o_p = _dsa_prompt_attn(_q, _qi, _kw, _kwb, _kb, _vb, batch, seq, tq, tk, topk_p)
            kp.append(_k.reshape(batch, seq, N_HEADS, HEAD_DIM))
            vp.append(_v.reshape(batch, seq, N_HEADS, HEAD_DIM))
            kip.append(_kw[:, :IDX_DIM].reshape(batch, seq, IDX_DIM))

            q_s, k_s, v_s, _, _, qi_s, kw_s, kwb_s = _proj_rope(
                xs, norm_mix_g[i], wqkv, wqi, wkw, tab_s, db)
            wi3 = kw_s[:, IDX_DIM:IDX_DIM + N_IDX_HEADS].reshape(db, N_IDX_HEADS, 1)
            scores = _sample_scores(page_table, qi_s.reshape(db, N_IDX_HEADS, IDX_DIM), wi3,
                                    cache_kidx, j, 8)
            ki_tiled = jnp.tile(kwb_s[:, :IDX_DIM], (1, N_IDX_HEADS))
            idx_t = _sample_select(scores.reshape(db, past), qi_s, ki_tiled, kw_s, topk_s)
            idx = idx_t[:, :db].T
            k3 = k_s.reshape(db, N_HEADS, HEAD_DIM)
            v3 = v_s.reshape(db, N_HEADS, HEAD_DIM)
            ksel, vsel = _gather_rows(page_table, idx, cache_k, cache_v, k3, v3, j, topk_s)
            o_s = _sample_attend(q_s.reshape(db, N_HEADS, HEAD_DIM), ksel, vsel).reshape(db, d)
            ksm.append(k3.reshape(db, ds, N_HEADS, HEAD_DIM))
            vsm.append(v3.reshape(db, ds, N_HEADS, HEAD_DIM))
            kism.append(kw_s[:, :IDX_DIM].reshape(db, ds, IDX_DIM))
        else:
            win = w_sg_in[j].astype(BF16)
            wo = w_sg_out[j].astype(BF16)
            gd = sg_ln_g.shape[1] // N_SG_GROUPS
            bs_full = jnp.repeat(b_spatial[j].T, gd, axis=1)
            o_p = _sgu_prompt(xp, norm_mix_g[i], win, sg_ln_g[j], sg_ln_b[j], w_spatial[j],
                              bs_full, tm_sgu)
            w0_row = jnp.repeat(w_spatial[j][:, 0, 0], gd)
            b0_row = jnp.repeat(b_spatial[j][:, 0], gd)
            o_s, vn_s = _sgu_sample(xs, norm_mix_g[i], win, sg_ln_g[j], sg_ln_b[j], w0_row, b0_row)
            svs.append(vn_s.reshape(db, ds, -1))
        xp = _post_ffn(xp, o_p, wo, norm_ffn_g[i], w1, w2, tm_ffn, tf_ffn)
        xs = _post_ffn(xs, o_s, wo, norm_ffn_g[i], w1, w2, db, tf_ffn)

    y_p = _final_norm(xp, final_norm_g, 512).reshape(batch, seq, d)
    y_s = _final_norm(xs, final_norm_g, db).reshape(db, ds, d)
    return (y_p, y_s, jnp.stack(kp), jnp.stack(vp), jnp.stack(kip),
            jnp.stack(ksm), jnp.stack(vsm), jnp.stack(kism), jnp.stack(svs))
```

```python
import functools
import math

import jax
import jax.numpy as jnp
import numpy as np
from jax import lax
from jax.experimental import pallas as pl
from jax.experimental.pallas import tpu as pltpu

F32 = jnp.float32
BF16 = jnp.bfloat16
I32 = jnp.int32

N_HEADS = 8
HEAD_DIM = 128
N_IDX_HEADS = 8
IDX_DIM = 64
TOPK_MAX = 256
ROPE_THETA = 10000.0
PAGE_SIZE = 128
CHUNK = 128
N_SG_GROUPS = 8
EPS = 1e-6

LANES = 128
INT_MIN = -(2 ** 31)
NEG_BIG = -1e30
VMEM_LIMIT = 48 * 1024 * 1024

_NT = (((1,), (1,)), ((), ()))


def _dot(a, b):
    return jnp.dot(a, b, preferred_element_type=F32)


def _dot_nt(a, b):
    return lax.dot_general(a, b, _NT, preferred_element_type=F32)


def _rms(x, g):
    y = x * lax.rsqrt(jnp.mean(x * x, axis=-1, keepdims=True) + EPS)
    return y * g


def _cparams(sem):
    return pltpu.CompilerParams(dimension_semantics=sem, vmem_limit_bytes=VMEM_LIMIT)


def _rope_tables(pos):
    pos = pos.astype(F32)[:, None]
    lane = jnp.arange(LANES)

    half = HEAD_DIM // 2
    inv = ROPE_THETA ** (-jnp.arange(half, dtype=F32) / half)
    ang = pos * inv[None, :]
    cos_h = jnp.concatenate([jnp.cos(ang), jnp.cos(ang)], axis=1)
    sin_h = jnp.concatenate([-jnp.sin(ang), jnp.sin(ang)], axis=1)

    half_i = IDX_DIM // 2
    inv_i = ROPE_THETA ** (-jnp.arange(half_i, dtype=F32) / half_i)
    ang_i = pos * inv_i[None, :]
    cos_i = jnp.tile(jnp.cos(ang_i), (1, LANES // half_i))
    sin_i = jnp.tile(jnp.sin(ang_i), (1, LANES // half_i))
    first = ((lane % IDX_DIM) < half_i)[None, :]
    sin_a = jnp.where(first, -sin_i, 0.0)
    sin_b = jnp.where(first, 0.0, sin_i)

    is_k = (lane < IDX_DIM)[None, :]
    is_w = ((lane >= IDX_DIM) & (lane < IDX_DIM + N_IDX_HEADS))[None, :]
    w_scale = float((N_IDX_HEADS * IDX_DIM) ** -0.5)
    cos_k = jnp.where(is_k, cos_i, jnp.where(is_w, w_scale, 0.0))
    sin_ka = jnp.where(is_k, sin_a, 0.0)
    sin_kb = jnp.where(is_k, sin_b, 0.0)
    return jnp.stack([cos_h, sin_h, cos_i, sin_a, sin_b, cos_k, sin_ka, sin_kb]).astype(F32)


def _proj_rope_body(x_ref, g_ref, wqkv_ref, wqi_ref, wkw_ref, tab_ref, *out_refs, for_prompt):
    if for_prompt:
        q_ref, k_ref, v_ref, qi_ref, kw_ref, kwb_ref, kb_ref, vt_ref = out_refs
    else:
        q_ref, k_ref, v_ref, qi_ref, kw_ref, kwb_ref = out_refs
    d_model = x_ref.shape[1]
    hn = _rms(x_ref[...], g_ref[...]).astype(BF16)
    cos_h, sin_h = tab_ref[0], tab_ref[1]
    pair = 2 * HEAD_DIM
    q_scale = np.float32(HEAD_DIM ** -0.5 * math.log2(math.e))

    def rope_h(z):
        return z * cos_h + pltpu.roll(z, HEAD_DIM // 2, 1) * sin_h

    for jp in range(N_HEADS // 2):
        zq = _dot(hn, wqkv_ref[:, jp * pair:(jp + 1) * pair])
        zk = _dot(hn, wqkv_ref[:, d_model + jp * pair:d_model + (jp + 1) * pair])
        zv = _dot(hn, wqkv_ref[:, 2 * d_model + jp * pair:2 * d_model + (jp + 1) * pair])
        for e in range(2):
            h = 2 * jp + e
            sl = slice(h * HEAD_DIM, (h + 1) * HEAD_DIM)
            es = slice(e * HEAD_DIM, (e + 1) * HEAD_DIM)
            qr = rope_h(zq[:, es])
            q_ref[:, sl] = (qr * q_scale if for_prompt else qr).astype(BF16)
            kr = rope_h(zk[:, es])
            k_ref[:, sl] = kr
            v_ref[:, sl] = zv[:, es]
            if for_prompt:
                kb_ref[:, sl] = kr.astype(BF16)
                vt_ref[sl, :] = zv[:, es].T.astype(BF16)

    def rope_idx(z, c, sa, sb):
        return (z * c + pltpu.roll(z, LANES - IDX_DIM // 2, 1) * sa
                + pltpu.roll(z, IDX_DIM // 2, 1) * sb)

    cos_i, sin_a, sin_b = tab_ref[2], tab_ref[3], tab_ref[4]
    for jp in range(N_IDX_HEADS // 2):
        z = _dot(hn, wqi_ref[:, jp * pair:(jp + 1) * pair])
        for e in range(2):
            sl = slice((2 * jp + e) * LANES, (2 * jp + e + 1) * LANES)
            qi_ref[:, sl] = rope_idx(z[:, e * LANES:(e + 1) * LANES], cos_i, sin_a, sin_b).astype(BF16)
    kw = rope_idx(_dot(hn, wkw_ref[...]), tab_ref[5], tab_ref[6], tab_ref[7])
    kw_ref[...] = kw
    kwb_ref[...] = kw.astype(BF16)


def _proj_rope(x2d, g, wqkv, wqi, wkw, tab, tm, for_prompt):
    n, d = x2d.shape
    nt = tab.shape[1] // tm
    row = lambda i: (i, 0)
    const = lambda i: (0, 0)
    n_qi = N_IDX_HEADS * LANES
    out_shape = [
        jax.ShapeDtypeStruct((n, d), BF16),
        jax.ShapeDtypeStruct((n, d), F32),
        jax.ShapeDtypeStruct((n, d), F32),
        jax.ShapeDtypeStruct((n, n_qi), BF16),
        jax.ShapeDtypeStruct((n, LANES), F32),
        jax.ShapeDtypeStruct((n, LANES), BF16),
    ]
    out_specs = [
        pl.BlockSpec((tm, d), row), pl.BlockSpec((tm, d), row), pl.BlockSpec((tm, d), row),
        pl.BlockSpec((tm, n_qi), row), pl.BlockSpec((tm, LANES), row), pl.BlockSpec((tm, LANES), row),
    ]
    if for_prompt:
        out_shape += [jax.ShapeDtypeStruct((n, d), BF16),
                      jax.ShapeDtypeStruct((d, n), BF16)]
        out_specs += [pl.BlockSpec((tm, d), row), pl.BlockSpec((d, tm), lambda i: (0, i))]
    return pl.pallas_call(
        functools.partial(_proj_rope_body, for_prompt=for_prompt),
        grid=(n // tm,),
        in_specs=[
            pl.BlockSpec((tm, d), row),
            pl.BlockSpec((1, d), const),
            pl.BlockSpec(wqkv.shape, const),
            pl.BlockSpec(wqi.shape, const),
            pl.BlockSpec(wkw.shape, const),
            pl.BlockSpec((8, tm, LANES), lambda i: (0, i % nt, 0)),
        ],
        out_specs=tuple(out_specs),
        out_shape=tuple(out_shape),
        compiler_params=_cparams(("arbitrary",)),
        name="dsa_proj_rope",
    )(x2d, g.reshape(1, d), wqkv, wqi, wkw, tab)


def _post_ffn_body(x_ref, o_ref, wo_ref, g_ref, w1_ref, w2_ref, out_ref, x1_scr, hn_scr, acc_scr):
    f = pl.program_id(1)

    @pl.when(f == 0)
    def _():
        x1 = x_ref[...] + _dot(o_ref[...], wo_ref[...])
        x1_scr[...] = x1
        hn_scr[...] = _rms(x1, g_ref[...]).astype(BF16)
        acc_scr[...] = jnp.zeros_like(acc_scr)

    h1 = jnp.square(jnp.maximum(_dot(hn_scr[...], w1_ref[...]), 0.0)).astype(BF16)
    acc_scr[...] += _dot(h1, w2_ref[...])

    @pl.when(f == pl.num_programs(1) - 1)
    def _():
        out_ref[...] = x1_scr[...] + acc_scr[...]


def _post_ffn(x2d, o2d, wo, g, w1, w2, tm, tf):
    n, d = x2d.shape
    dff = w1.shape[1]
    row = lambda i, f: (i, 0)
    return pl.pallas_call(
        _post_ffn_body,
        grid=(n // tm, dff // tf),
        in_specs=[
            pl.BlockSpec((tm, d), row),
            pl.BlockSpec((tm, d), row),
            pl.BlockSpec(wo.shape, lambda i, f: (0, 0)),
            pl.BlockSpec((1, d), lambda i, f: (0, 0)),
            pl.BlockSpec((d, tf), lambda i, f: (0, f)),
            pl.BlockSpec((tf, d), lambda i, f: (f, 0)),
        ],
        out_specs=pl.BlockSpec((tm, d), row),
        out_shape=jax.ShapeDtypeStruct((n, d), F32),
        scratch_shapes=[pltpu.VMEM((tm, d), F32), pltpu.VMEM((tm, d), BF16),
                        pltpu.VMEM((tm, d), F32)],
        compiler_params=_cparams(("arbitrary", "arbitrary")),
        name="outproj_ffn",
    )(x2d, o2d, wo, g.reshape(1, d), w1, w2)


def _sgu_core(x_ref, g_ref, win_ref, lng_ref, lnb_ref):
    d_gate = lng_ref.shape[1]
    hn = _rms(x_ref[...], g_ref[...]).astype(BF16)
    z = _dot(hn, win_ref[...])
    z = 0.5 * z * (1.0 + lax.erf(z * np.float32(math.sqrt(0.5))))
    u, v = z[:, :d_gate], z[:, d_gate:]
    vc = v - jnp.mean(v, axis=-1, keepdims=True)
    var = jnp.mean(vc * vc, axis=-1, keepdims=True)
    vn = vc * lax.rsqrt(var + EPS) * lng_ref[...] + lnb_ref[...]
    return u, vn


def _sgu_prompt_body(x_ref, g_ref, win_ref, lng_ref, lnb_ref, ws_ref, bs_ref, o_ref):
    u, vn = _sgu_core(x_ref, g_ref, win_ref, lng_ref, lnb_ref)
    vb = vn.astype(BF16)
    gd = vn.shape[1] // N_SG_GROUPS
    r = lax.broadcasted_iota(I32, (CHUNK, CHUNK), 0)
    c = lax.broadcasted_iota(I32, (CHUNK, CHUNK), 1)
    tril = c <= r
    for g in range(N_SG_GROUPS):
        wm = jnp.where(tril, ws_ref[g], 0.0).astype(BF16)
        gs = slice(g * gd, (g + 1) * gd)
        for ch in range(x_ref.shape[0] // CHUNK):
            rs = slice(ch * CHUNK, (ch + 1) * CHUNK)
            mix = _dot(wm, vb[rs, gs]) + bs_ref[:, gs]
            o_ref[rs, gs] = (u[rs, gs] * mix).astype(BF16)


def _sgu_prompt(x2d, g, win, lng, lnb, ws, bs_full, tm):
    n, d = x2d.shape
    dg = lng.shape[0]
    row = lambda i: (i, 0)
    const = lambda i: (0, 0)
    return pl.pallas_call(
        _sgu_prompt_body,
        grid=(n // tm,),
        in_specs=[
            pl.BlockSpec((tm, d), row),
            pl.BlockSpec((1, d), const),
            pl.BlockSpec(win.shape, const),
            pl.BlockSpec((1, dg), const),
            pl.BlockSpec((1, dg), const),
            pl.BlockSpec(ws.shape, lambda i: (0, 0, 0)),
            pl.BlockSpec(bs_full.shape, const),
        ],
        out_specs=pl.BlockSpec((tm, dg), row),
        out_shape=jax.ShapeDtypeStruct((n, dg), BF16),
        compiler_params=_cparams(("arbitrary",)),
        name="sgu_prompt",
    )(x2d, g.reshape(1, d), win, lng.reshape(1, dg), lnb.reshape(1, dg), ws, bs_full)


def _sgu_sample_body(x_ref, g_ref, win_ref, lng_ref, lnb_ref, w0_ref, b0_ref, o_ref, vn_ref):
    u, vn = _sgu_core(x_ref, g_ref, win_ref, lng_ref, lnb_ref)
    vn_ref[...] = vn
    o_ref[...] = (u * (w0_ref[...] * vn + b0_ref[...])).astype(BF16)


def _sgu_sample(x2d, g, win, lng, lnb, w0_row, b0_row):
    n, d = x2d.shape
    dg = lng.shape[0]
    full = lambda a: pl.BlockSpec(a.shape, lambda i: (0,) * a.ndim)
    args = (x2d, g.reshape(1, d), win, lng.reshape(1, dg), lnb.reshape(1, dg),
            w0_row.reshape(1, dg), b0_row.reshape(1, dg))
    return pl.pallas_call(
        _sgu_sample_body,
        grid=(1,),
        in_specs=[full(a) for a in args],
        out_specs=(pl.BlockSpec((n, dg), lambda i: (0, 0)), pl.BlockSpec((n, dg), lambda i: (0, 0))),
        out_shape=(jax.ShapeDtypeStruct((n, dg), BF16), jax.ShapeDtypeStruct((n, dg), F32)),
        compiler_params=_cparams(("arbitrary",)),
        name="sgu_sample",
    )(*args)


def _norm_body(x_ref, g_ref, o_ref):
    o_ref[...] = _rms(x_ref[...], g_ref[...])


def _final_norm(x2d, g, tm):
    n, d = x2d.shape
    return pl.pallas_call(
        _norm_body,
        grid=(n // tm,),
        in_specs=[pl.BlockSpec((tm, d), lambda i: (i, 0)), pl.BlockSpec((1, d), lambda i: (0, 0))],
        out_specs=pl.BlockSpec((tm, d), lambda i: (i, 0)),
        out_shape=jax.ShapeDtypeStruct((n, d), F32),
        compiler_params=_cparams(("arbitrary",)),
        name="final_norm",
    )(x2d, g.reshape(1, d))


def _score_keys(score):
    bits = lax.bitcast_convert_type(score, I32)
    return jnp.where(bits >= 0, bits, INT_MIN - bits)


def _bit(n):
    return jnp.left_shift(jnp.int32(1), n)


def _dsa_prompt_body(qb_tab, kb_tab, q_ref, qi_ref, wit_ref, kiall_ref, k_ref, vt_ref, o_ref,
                     keys_scr, m_scr, l_scr, acc_scr, *, tq, tk, topk, seq):
    p = pl.program_id(1)
    qb = qb_tab[p]
    kb = kb_tab[p]
    n_chunks = (qb * tq + tq + tk - 1) // tk
    idx_bits = max(1, (seq - 1).bit_length())
    sub = 8

    @pl.when(kb == 0)
    def _select():
        wit = wit_ref[...]
        kidx = lax.broadcasted_iota(I32, (tk, tq), 0)
        qpos = qb * tq + lax.broadcasted_iota(I32, (tk, tq), 1)

        def score_chunk(c, carry):
            s0 = pl.multiple_of(c * tk, tk)
            kic = kiall_ref[pl.ds(s0, tk), :]
            acc = jnp.zeros((tk, tq), F32)
            for h in range(N_IDX_HEADS):
                s = _dot_nt(kic, qi_ref[:, h * LANES:(h + 1) * LANES])
                acc = acc + jnp.maximum(s, 0.0) * wit[h:h + 1, :]
            keys_scr[pl.ds(s0, tk), :] = jnp.where(kidx + s0 <= qpos, _score_keys(acc), INT_MIN)
            return carry

        lax.fori_loop(0, n_chunks, score_chunk, 0)

        def count(pred):
            def body(c, cnt):
                s0 = pl.multiple_of(c * tk, tk)
                hit = pred(keys_scr[pl.ds(s0, tk), :], s0)
                return cnt + jnp.sum(hit.reshape(tk // sub, sub, tq), axis=0)

            cnt = lax.fori_loop(0, n_chunks, body, jnp.zeros((sub, tq), I32))
            return jnp.sum(cnt, axis=0, keepdims=True)

        def ge(cand):
            return lambda blk, s0: jnp.where(blk >= cand, 1, 0)

        def search_cond(st):
            i, _, done = st
            return jnp.logical_and(i < 32, jnp.min(done) == 0)

        def search_step(st):
            i, lo, done = st
            for u in range(4):
                cand = lo + _bit(31 - (i + u))
                cnt = count(ge(cand))
                lo = jnp.where(done == 0, jnp.where(cnt >= topk, cand, lo), lo)
                done = jnp.where(cnt == topk, 1, done)
            return i + 4, lo, done

        _, lo, done = lax.while_loop(
            search_cond, search_step,
            (jnp.int32(0), jnp.full((1, tq), INT_MIN, I32), jnp.zeros((1, tq), I32)))

        split = jnp.where(done == 0, jnp.where(lo > INT_MIN, 1, 0), 0)

        def tie_search():
            need = topk - count(lambda blk, s0: jnp.where(blk > lo, 1, 0))

            def step(i, pfx):
                cand = pfx + _bit(idx_bits - 1 - i)
                cnt = count(lambda blk, s0: jnp.where(
                    jnp.where(blk == lo, kidx + s0, seq) < cand, 1, 0))
                return jnp.where(cnt < need, cand, pfx)

            return lax.fori_loop(0, idx_bits, step, jnp.zeros((1, tq), I32))

        cut = lax.cond(jnp.max(split) > 0, tie_search, lambda: jnp.full((1, tq), seq, I32))
        cut = jnp.where(split > 0, cut, seq)

        def bias_chunk(c, carry):
            s0 = pl.multiple_of(c * tk, tk)
            blk = keys_scr[pl.ds(s0, tk), :]
            tie = jnp.where(kidx + s0 <= cut, 0.0, NEG_BIG)
            bias = jnp.where(blk > lo, 0.0, jnp.where(blk == lo, tie, NEG_BIG))
            bias = jnp.where(blk == INT_MIN, NEG_BIG, bias).astype(F32)
            keys_scr[pl.ds(s0, tk), :] = lax.bitcast_convert_type(bias, I32)
            return carry

        lax.fori_loop(0, n_chunks, bias_chunk, 0)
        m_scr[...] = jnp.full(m_scr.shape, NEG_BIG, F32)
        l_scr[...] = jnp.zeros(l_scr.shape, F32)
        acc_scr[...] = jnp.zeros(acc_scr.shape, F32)

    bias = lax.bitcast_convert_type(keys_scr[pl.ds(pl.multiple_of(kb * tk, tk), tk), :], F32)
    def qk(h):
        hs = slice(h * HEAD_DIM, (h + 1) * HEAD_DIM)
        return _dot_nt(k_ref[:, hs], q_ref[:, hs])

    s_next = qk(0)
    for h in range(N_HEADS):
        hs = slice(h * HEAD_DIM, (h + 1) * HEAD_DIM)
        s_raw = s_next
        if h + 1 < N_HEADS:
            s_next = qk(h + 1)
        s = bias + s_raw
        m_old = m_scr[h:h + 1, :]
        m_new = jnp.maximum(m_old, jnp.max(s, axis=0, keepdims=True))
        alpha = jnp.exp2(m_old - m_new)
        pexp = jnp.exp2(s - m_new)
        l_scr[h:h + 1, :] = alpha * l_scr[h:h + 1, :] + jnp.sum(pexp, axis=0, keepdims=True)
        acc_scr[hs, :] = alpha * acc_scr[hs, :] + _dot(vt_ref[hs, :], pexp.astype(BF16))
        m_scr[h:h + 1, :] = m_new

    @pl.when(kb == n_chunks - 1)
    def _finish():
        for h in range(N_HEADS):
            hs = slice(h * HEAD_DIM, (h + 1) * HEAD_DIM)
            o_ref[:, hs] = (acc_scr[hs, :] / l_scr[h:h + 1, :]).T.astype(BF16)


def _dsa_prompt_attn(q, qi, wit, kwb, kb16, vt16, batch, seq, tq, tk, topk):
    d = q.shape[1]
    nqb, nkb = seq // tq, seq // tk
    pairs = [(a, b) for a in range(nqb) for b in range(((a + 1) * tq - 1) // tk + 1)]
    qb_tab = jnp.asarray([a for a, _ in pairs], I32)
    kb_tab = jnp.asarray([b for _, b in pairs], I32)
    qrow = lambda b, p, qt, kt: (b * nqb + qt[p], 0)
    qcol = lambda b, p, qt, kt: (0, b * nqb + qt[p])
    krow = lambda b, p, qt, kt: (b * nkb + kt[p], 0)
    kcol = lambda b, p, qt, kt: (0, b * nkb + kt[p])
    grid_spec = pltpu.PrefetchScalarGridSpec(
        num_scalar_prefetch=2,
        grid=(batch, len(pairs)),
        in_specs=[
            pl.BlockSpec((tq, d), qrow),
            pl.BlockSpec((tq, qi.shape[1]), qrow),
            pl.BlockSpec((N_IDX_HEADS, tq), qcol),
            pl.BlockSpec((seq, LANES), lambda b, p, qt, kt: (b, 0)),
            pl.BlockSpec((tk, d), krow),
            pl.BlockSpec((d, tk), kcol),
        ],
        out_specs=pl.BlockSpec((tq, d), qrow),
        scratch_shapes=[
            pltpu.VMEM((seq, tq), I32),
            pltpu.VMEM((N_HEADS, tq), F32),
            pltpu.VMEM((N_HEADS, tq), F32),
            pltpu.VMEM((d, tq), F32),
        ],
    )
    return pl.pallas_call(
        functools.partial(_dsa_prompt_body, tq=tq, tk=tk, topk=topk, seq=seq),
        grid_spec=grid_spec,
        out_shape=jax.ShapeDtypeStruct((batch * seq, d), BF16),
        compiler_params=_cparams(("arbitrary", "arbitrary")),
        name="dsa_prompt_attn",
    )(qb_tab, kb_tab, q, qi, wit, kwb, kb16, vt16)


def _sample_scores_body(pt_ref, qi_ref, wi_ref, *rest, n_pg):
    page_refs, out_ref = rest[:n_pg], rest[n_pg]
    qi = qi_ref[...]
    wi = wi_ref[...]
    for j in range(n_pg):
        s = _dot_nt(qi, page_refs[j][...].astype(BF16))
        out_ref[j:j + 1, :] = jnp.sum(jnp.maximum(s, 0.0) * wi, axis=0, keepdims=True)


def _sample_scores(page_table, qi3, wi3, cache_kidx, layer, n_pg):
    db, n_pages = page_table.shape

    def page_spec(j):
        return pl.BlockSpec((None, None, PAGE_SIZE, IDX_DIM),
                            lambda b, g, pt: (layer, pt[b, g * n_pg + j], 0, 0))

    grid_spec = pltpu.PrefetchScalarGridSpec(
        num_scalar_prefetch=1,
        grid=(db, n_pages // n_pg),
        in_specs=[
            pl.BlockSpec((None, N_IDX_HEADS, IDX_DIM), lambda b, g, pt: (b, 0, 0)),
            pl.BlockSpec((None, N_IDX_HEADS, 1), lambda b, g, pt: (b, 0, 0)),
        ] + [page_spec(j) for j in range(n_pg)],
        out_specs=pl.BlockSpec((None, n_pg, PAGE_SIZE), lambda b, g, pt: (b, g, 0)),
    )
    return pl.pallas_call(
        functools.partial(_sample_scores_body, n_pg=n_pg),
        grid_spec=grid_spec,
        out_shape=jax.ShapeDtypeStruct((db, n_pages, PAGE_SIZE), F32),
        compiler_params=_cparams(("arbitrary", "arbitrary")),
        name="sample_scores",
    )(page_table, qi3, wi3, *([cache_kidx] * n_pg))


def _sample_select_body(sc_ref, qi_ref, kit_ref, kw_ref, idx_ref, keys_scr, c_scr, *, topk):
    rows, width = sc_ref.shape
    idx_bits = max(1, (width - 1).bit_length())

    prod = qi_ref[...].astype(F32) * kit_ref[...].astype(F32)
    head_of = lax.broadcasted_iota(I32, prod.shape, 1) // LANES
    s_new = jnp.zeros((rows, 1), F32)
    for h in range(N_IDX_HEADS):
        qk = jnp.sum(jnp.where(head_of == h, prod, 0.0), axis=1, keepdims=True)
        s_new = s_new + jnp.maximum(qk, 0.0) * kw_ref[:, IDX_DIM + h:IDX_DIM + h + 1]
    key_new = _score_keys(s_new)
    keys_scr[...] = _score_keys(sc_ref[...])

    def count(past_hit, new_hit):
        return jnp.sum(past_hit.astype(I32), axis=1, keepdims=True) + new_hit.astype(I32)

    c0 = count(keys_scr[...] >= 0, key_new >= 0)
    lo = jnp.where(c0 >= topk, 0, INT_MIN).astype(I32)

    def bit_step(i, lo):
        cand = lo + _bit(30 - i)
        cnt = count(keys_scr[...] >= cand, key_new >= cand)
        return jnp.where(cnt >= topk, cand, lo)

    lo = lax.fori_loop(0, 31, bit_step, lo)
    need = topk - count(keys_scr[...] > lo, key_new > lo)
    idx = lax.broadcasted_iota(I32, (rows, width), 1)

    def tie_step(i, pfx):
        cand = pfx + _bit(idx_bits - 1 - i)
        hit = jnp.where(keys_scr[...] == lo, idx, width) < cand
        cnt = jnp.sum(hit.astype(I32), axis=1, keepdims=True)
        return jnp.where(cnt < need, cand, pfx)

    cut = lax.fori_loop(0, idx_bits, tie_step, jnp.zeros((rows, 1), I32))
    keys = keys_scr[...]
    tie = jnp.where(idx <= cut, 1, 0)
    keys_scr[...] = jnp.where(keys > lo, 1, jnp.where(keys == lo, tie, 0)).astype(I32)

    r_i = lax.broadcasted_iota(I32, (LANES, LANES), 0)
    c_i = lax.broadcasted_iota(I32, (LANES, LANES), 1)
    upper = (r_i <= c_i).astype(BF16)

    ones = jnp.ones((LANES, LANES), BF16)

    def csum(j, off):
        sl = pl.ds(pl.multiple_of(j * LANES, LANES), LANES)
        mb = keys_scr[:, sl].astype(BF16)
        c_scr[:, sl] = _dot(mb, upper) + off
        return off + _dot(mb, ones)

    lax.fori_loop(0, width // LANES, csum, jnp.zeros((rows, LANES), F32))

    r_col = lax.broadcasted_iota(I32, (topk, LANES), 0).astype(F32)
    lane_b = lax.broadcasted_iota(I32, (topk, LANES), 1)

    sub = 8

    def per_group(gi, out):
        g0 = pl.multiple_of(gi * sub, sub)
        for i in range(sub):
            def blk(j, acc, i=i):
                c8 = c_scr[pl.ds(g0, sub), pl.ds(pl.multiple_of(j * LANES, LANES), LANES)]
                return acc + (c8[i:i + 1, :] <= r_col).astype(F32)

            acc = lax.fori_loop(0, width // LANES, blk, jnp.zeros((topk, LANES), F32))
            col = jnp.sum(acc, axis=1, keepdims=True).astype(I32)
            out = jnp.where(lane_b == g0 + i, col, out)
        return out

    idx_ref[...] = lax.fori_loop(0, rows // sub, per_group, jnp.zeros((topk, LANES), I32))


def _sample_select(scores, qi, ki_tiled, kw, topk):
    rows, width = scores.shape
    args = (scores, qi, ki_tiled, kw)
    return pl.pallas_call(
        functools.partial(_sample_select_body, topk=topk),
        grid=(1,),
        in_specs=[pl.BlockSpec(a.shape, lambda i: (0, 0)) for a in args],
        out_specs=pl.BlockSpec((topk, LANES), lambda i: (0, 0)),
        out_shape=jax.ShapeDtypeStruct((topk, LANES), I32),
        scratch_shapes=[pltpu.VMEM((rows, width), I32), pltpu.VMEM((rows, width), F32)],
        compiler_params=_cparams(("arbitrary",)),
        name="sample_select",
    )(*args)


def _gather_rows_body(pt_ref, idx_ref, ck_ref, cv_ref, kn_ref, vn_ref, ko_ref, vo_ref, sem,
                      *, layer, past, topk):
    b = pl.program_id(0)

    def copies(r):
        i = jnp.minimum(idx_ref[b, r], past - 1)
        phys = pt_ref[b, i // PAGE_SIZE]
        off = i % PAGE_SIZE
        return (pltpu.make_async_copy(ck_ref.at[layer, phys, off], ko_ref.at[0, r], sem.at[0]),
                pltpu.make_async_copy(cv_ref.at[layer, phys, off], vo_ref.at[0, r], sem.at[1]))

    def start(r, carry):
        ck, cv = copies(r)
        ck.start()
        cv.start()
        return carry

    def wait(r, carry):
        ck, cv = copies(r)
        ck.wait()
        cv.wait()
        return carry

    lax.fori_loop(0, topk, start, 0, unroll=8)
    lax.fori_loop(0, topk, wait, 0, unroll=8)

    @pl.when(idx_ref[b, topk - 1] >= past)
    def _():
        ko_ref[0, topk - 1] = kn_ref[0]
        vo_ref[0, topk - 1] = vn_ref[0]


def _gather_rows(page_table, idx, cache_k, cache_v, k_new, v_new, layer, topk):
    db, n_pages = page_table.shape
    row_shape = cache_k.shape[3:]
    any_spec = pl.BlockSpec(memory_space=pl.ANY)
    new_spec = pl.BlockSpec((1,) + row_shape, lambda b, pt, ix: (b, 0, 0))
    out_spec = pl.BlockSpec((1, topk) + row_shape, lambda b, pt, ix: (b, 0, 0, 0))
    grid_spec = pltpu.PrefetchScalarGridSpec(
        num_scalar_prefetch=2,
        grid=(db,),
        in_specs=[any_spec, any_spec, new_spec, new_spec],
        out_specs=(out_spec, out_spec),
        scratch_shapes=[pltpu.SemaphoreType.DMA((2,))],
    )
    out = jax.ShapeDtypeStruct((db, topk) + row_shape, cache_k.dtype)
    return pl.pallas_call(
        functools.partial(_gather_rows_body, layer=layer, past=n_pages * PAGE_SIZE, topk=topk),
        grid_spec=grid_spec,
        out_shape=(out, out),
        compiler_params=_cparams(("arbitrary",)),
        name="sample_gather",
    )(page_table, idx, cache_k, cache_v, k_new, v_new)


def _sample_attend_body(q_ref, k_ref, v_ref, o_ref):
    q = q_ref[...].astype(F32)
    s = jnp.sum(k_ref[...] * q[None], axis=2, keepdims=True) * np.float32(HEAD_DIM ** -0.5)
    pexp = jnp.exp(s - jnp.max(s, axis=0, keepdims=True))
    prob = pexp / jnp.sum(pexp, axis=0, keepdims=True)
    o_ref[...] = jnp.sum(prob * v_ref[...], axis=0).astype(BF16)


def _sample_attend(q3, ksel, vsel):
    db, topk = ksel.shape[:2]
    row_shape = ksel.shape[2:]
    vec = pl.BlockSpec((None,) + row_shape, lambda b: (b, 0, 0))
    sel = pl.BlockSpec((None, topk) + row_shape, lambda b: (b, 0, 0, 0))
    return pl.pallas_call(
        _sample_attend_body,
        grid=(db,),
        in_specs=[vec, sel, sel],
        out_specs=vec,
        out_shape=jax.ShapeDtypeStruct((db,) + row_shape, BF16),
        compiler_params=_cparams(("arbitrary",)),
        name="sample_attend",
    )(q3, ksel, vsel)


def kernel(x_prompt, x_sample, cache_k, cache_v, cache_kidx, page_table, norm_mix_g, norm_ffn_g,
           final_norm_g, w_attn_in, w_attn_out, w_sg_in, sg_ln_g, sg_ln_b, w_spatial, b_spatial,
           w_sg_out, w_ff1, w_ff2):
    batch, seq, d = x_prompt.shape
    db, ds, _ = x_sample.shape
    assert ds == 1 and d == N_HEADS * HEAD_DIM
    depth = norm_mix_g.shape[0]
    n_pages = page_table.shape[1]
    past = n_pages * PAGE_SIZE
    n_qi = N_IDX_HEADS * IDX_DIM
    topk_p = min(TOPK_MAX, seq // 4)
    topk_s = min(TOPK_MAX, (past + ds) // 4)

    xp = x_prompt.reshape(batch * seq, d)
    xs = x_sample.reshape(db, d)
    tab_p = _rope_tables(jnp.arange(seq))
    tab_s = _rope_tables(jnp.full((db,), past))

    tm_proj, tm_ffn, tf_ffn, tm_sgu, tq, tk = 256, 512, 1024, 256, 256, 512
    pages_per_step = math.gcd(n_pages, 32)
    kp, vp, kip, ksm, vsm, kism, svs = [], [], [], [], [], [], []
    for i in range(depth):
        j = i // 2
        w1 = w_ff1[i].astype(BF16)
        w2 = w_ff2[i].astype(BF16)
        if i % 2 == 0:
            w_in = w_attn_in[j]
            wqkv = w_in[:, :3 * d].astype(BF16)
            wqi = w_in[:, 3 * d:3 * d + n_qi].reshape(d, N_IDX_HEADS, IDX_DIM)
            wqi = jnp.pad(wqi, ((0, 0), (0, 0), (0, LANES - IDX_DIM)))
            wqi = wqi.reshape(d, N_IDX_HEADS * LANES).astype(BF16)
            wkw = w_in[:, 3 * d + n_qi:]
            wkw = jnp.pad(wkw, ((0, 0), (0, LANES - wkw.shape[1]))).astype(BF16)
            wo = w_attn_out[j].astype(BF16)

            _q, _k, _v, _qi, _kw, _kwb, _kb, _vt = _proj_rope(
                xp, norm_mix_g[i], wqkv, wqi, wkw, tab_p, tm_proj, True)
            wit = _kw[:, IDX_DIM:IDX_DIM + N_IDX_HEADS].T
            o_p = _dsa_prompt_attn(_q, _qi, wit, _kwb, _kb, _vt, batch, seq, tq, tk, topk_p)
            kp.append(_k.reshape(batch, seq, N_HEADS, HEAD_DIM))
            vp.append(_v.reshape(batch, seq, N_HEADS, HEAD_DIM))
            kip.append(_kw[:, :IDX_DIM].reshape(batch, seq, IDX_DIM))

            q_s, k_s, v_s, qi_s, kw_s, kwb_s = _proj_rope(
                xs, norm_mix_g[i], wqkv, wqi, wkw, tab_s, db, False)
            wi3 = kw_s[:, IDX_DIM:IDX_DIM + N_IDX_HEADS].reshape(db, N_IDX_HEADS, 1)
            qi3 = qi_s.reshape(db, N_IDX_HEADS, LANES)[:, :, :IDX_DIM]
            scores = _sample_scores(page_table, qi3, wi3, cache_kidx, j, pages_per_step)
            ki_tiled = jnp.tile(kwb_s, (1, N_IDX_HEADS))
            idx_t = _sample_select(scores.reshape(db, past), qi_s, ki_tiled, kw_s, topk_s)
            idx = idx_t[:, :db].T
            k3 = k_s.reshape(db, N_HEADS, HEAD_DIM)
            v3 = v_s.reshape(db, N_HEADS, HEAD_DIM)
            ksel, vsel = _gather_rows(page_table, idx, cache_k, cache_v, k3, v3, j, topk_s)
            o_s = _sample_attend(q_s.reshape(db, N_HEADS, HEAD_DIM), ksel, vsel).reshape(db, d)
            ksm.append(k3.reshape(db, ds, N_HEADS, HEAD_DIM))
            vsm.append(v3.reshape(db, ds, N_HEADS, HEAD_DIM))
            kism.append(kw_s[:, :IDX_DIM].reshape(db, ds, IDX_DIM))
        else:
            win = w_sg_in[j].astype(BF16)
            wo = w_sg_out[j].astype(BF16)
            gd = sg_ln_g.shape[1] // N_SG_GROUPS
            bs_full = jnp.repeat(b_spatial[j].T, gd, axis=1)
            o_p = _sgu_prompt(xp, norm_mix_g[i], win, sg_ln_g[j], sg_ln_b[j], w_spatial[j],
                              bs_full, tm_sgu)
            w0_row = jnp.repeat(w_spatial[j][:, 0, 0], gd)
            b0_row = jnp.repeat(b_spatial[j][:, 0], gd)
            o_s, vn_s = _sgu_sample(xs, norm_mix_g[i], win, sg_ln_g[j], sg_ln_b[j], w0_row, b0_row)
            svs.append(vn_s.reshape(db, ds, -1))
        xp = _post_ffn(xp, o_p, wo, norm_ffn_g[i], w1, w2, tm_ffn, tf_ffn)
        xs = _post_ffn(xs, o_s, wo, norm_ffn_g[i], w1, w2, db, tf_ffn)

    y_p = _final_norm(xp, final_norm_g, 512).reshape(batch, seq, d)
    y_s = _final_norm(xs, final_norm_g, db).reshape(db, ds, d)
    return (y_p, y_s, jnp.stack(kp), jnp.stack(vp), jnp.stack(kip),
            jnp.stack(ksm), jnp.stack(vsm), jnp.stack(kism), jnp.stack(svs))
```
